```python
import jax, jax.numpy as jnp
from jax import lax
import numpy as np

D_MODEL = 2048
BATCH = 8
SEQ = 2048
DEPTH = 4
DEC_BATCH = 16
DEC_SEQ = 64
PAST_LEN = 1024

CHUNK = 64
Q_BLOCK = 128
N_MIXERS = 2
N_MLA = (DEPTH + 1) // 2
N_SB = DEPTH // 2
MLA_HEADS = 16
Q_LORA = 512
KV_LORA = 512
NOPE_DIM = 128
ROPE_DIM = 64
V_DIM = 128
MLA_WIDTH = MLA_HEADS * V_DIM
MLA_IN = Q_LORA + KV_LORA + ROPE_DIM + MLA_WIDTH
ROPE_THETA = 10000.0
SB_HEADS = 16
SB_HEAD_DIM = 128
SB_WIDTH = SB_HEADS * SB_HEAD_DIM
SB_IN = 4 * SB_WIDTH
EPS = 1e-6
NEG = -1e30

kernel_name = "chunk_streaming_mla_stickbreaking_hybrid"


def rms_norm(x, g):
    xf = x.astype(jnp.float32)
    y = xf * lax.rsqrt(jnp.mean(xf * xf, axis=-1, keepdims=True) + EPS)
    return (y * g.astype(jnp.float32)).astype(x.dtype)


def rope(x, pos):
    half = ROPE_DIM // 2
    inv = 1.0 / (ROPE_THETA ** (jnp.arange(half, dtype=jnp.float32) * (2.0 / ROPE_DIM)))
    ang = pos.astype(jnp.float32)[:, None] * inv[None, :]
    shape = (1, pos.shape[0]) + (1,) * (x.ndim - 3) + (half,)
    cos = jnp.cos(ang).reshape(shape)
    sin = jnp.sin(ang).reshape(shape)
    xf = x.astype(jnp.float32)
    x1, x2 = xf[..., :half], xf[..., half:]
    return jnp.concatenate([x1 * cos - x2 * sin, x1 * sin + x2 * cos], axis=-1).astype(x.dtype)


def sweep_query_blocks(fn, q, q_pos):
    B, T = q.shape[0], q.shape[1]
    if T <= Q_BLOCK:
        return fn(q, q_pos)
    nb = T // Q_BLOCK
    qb = jnp.moveaxis(q.reshape((B, nb, Q_BLOCK) + q.shape[2:]), 1, 0)
    pb = q_pos.reshape(nb, Q_BLOCK)
    out = lax.map(lambda a: fn(a[0], a[1]), (qb, pb))
    return jnp.moveaxis(out, 0, 1).reshape((B, T) + out.shape[3:])


def mla_attend(q, k, v, q_pos, k_pos):
    s = jnp.einsum('bqhd,bkhd->bhqk', q, k).astype(jnp.float32) * ((NOPE_DIM + ROPE_DIM) ** -0.5)
    mask = (k_pos[None, :] // CHUNK) <= (q_pos[:, None] // CHUNK)
    s = jnp.where(mask[None, None], s, NEG)
    p = jax.nn.softmax(s, axis=-1).astype(v.dtype)
    return jnp.einsum('bhqk,bkhd->bqhd', p, v)


def sb_attend(q, k, v, q_pos, k_pos):
    z = jnp.einsum('bqhd,bkhd->bhqk', q, k).astype(jnp.float32) * (SB_HEAD_DIM ** -0.5)
    causal = (k_pos[None, :] < q_pos[:, None])[None, None]
    log_beta = jax.nn.log_sigmoid(z)
    log_fail = jnp.where(causal, jax.nn.log_sigmoid(-z), 0.0)
    later = lax.cumsum(log_fail, axis=3, reverse=True) - log_fail
    a = jnp.where(causal, jnp.exp(log_beta + later), 0.0).astype(v.dtype)
    return jnp.einsum('bhqk,bkhd->bqhd', a, v)


def mla_mixer(h, pos, ckv_past, kr_past, w_in, q_norm, kv_norm, w_q_up, w_kv_up, w_o):
    B, T, _ = h.shape
    proj = h @ w_in
    cq, ckv, kr, gate = jnp.split(proj, [Q_LORA, Q_LORA + KV_LORA, Q_LORA + KV_LORA + ROPE_DIM], axis=-1)
    cq = rms_norm(cq, q_norm)
    ckv = rms_norm(ckv, kv_norm)
    kr = rope(kr, pos)
    q = (cq @ w_q_up).reshape(B, T, MLA_HEADS, NOPE_DIM + ROPE_DIM)
    q = jnp.concatenate([q[..., :NOPE_DIM], rope(q[..., NOPE_DIM:], pos)], axis=-1)
    if ckv_past is None:
        ckv_all, kr_all, k_pos = ckv, kr, pos
    else:
        past_pos = jnp.arange(ckv_past.shape[1], dtype=jnp.int32)
        ckv_all = jnp.concatenate([ckv_past, ckv], axis=1)
        kr_all = jnp.concatenate([kr_past, kr], axis=1)
        k_pos = jnp.concatenate([past_pos, pos])
    Tk = ckv_all.shape[1]
    kv = (ckv_all @ w_kv_up).reshape(B, Tk, MLA_HEADS, NOPE_DIM + V_DIM)
    k = jnp.concatenate([kv[..., :NOPE_DIM],
                         jnp.broadcast_to(kr_all[:, :, None, :], (B, Tk, MLA_HEADS, ROPE_DIM))], axis=-1)
    v = kv[..., NOPE_DIM:]
    o = sweep_query_blocks(lambda qb, pb: mla_attend(qb, k, v, pb, k_pos), q, pos)
    o = o.reshape(B, T, MLA_WIDTH) * jax.nn.silu(gate)
    return o @ w_o, ckv, kr


def sb_mixer(h, pos, k_past, v_past, w_in, w_o):
    B, T, _ = h.shape
    q, k, v, gate = jnp.split(h @ w_in, 4, axis=-1)
    q = q.reshape(B, T, SB_HEADS, SB_HEAD_DIM)
    k = k.reshape(B, T, SB_HEADS, SB_HEAD_DIM)
    v = v.reshape(B, T, SB_HEADS, SB_HEAD_DIM)
    if k_past is None:
        k_all, v_all, k_pos = k, v, pos
    else:
        past_pos = jnp.arange(k_past.shape[1], dtype=jnp.int32)
        k_all = jnp.concatenate([k_past, k], axis=1)
        v_all = jnp.concatenate([v_past, v], axis=1)
        k_pos = jnp.concatenate([past_pos, pos])
    o = sweep_query_blocks(lambda qb, pb: sb_attend(qb, k_all, v_all, pb, k_pos), q, pos)
    o = o.reshape(B, T, SB_WIDTH) * jax.nn.silu(gate)
    return o @ w_o, k, v


def run_trunk(x, pos, past_ckv, past_kr, past_sk, past_sv, ln_gain, final_gain,
              mla_w_in, mla_q_norm, mla_kv_norm, mla_w_q_up, mla_w_kv_up, mla_w_o,
              sb_w_in, sb_w_o):
    new_ckv, new_kr, new_sk, new_sv = [], [], [], []
    for i in range(DEPTH):
        h = rms_norm(x, ln_gain[i])
        j = i // N_MIXERS
        if i % N_MIXERS == 0:
            out, ckv, kr = mla_mixer(h, pos,
                                     None if past_ckv is None else past_ckv[j],
                                     None if past_kr is None else past_kr[j],
                                     mla_w_in[j], mla_q_norm[j], mla_kv_norm[j],
                                     mla_w_q_up[j], mla_w_kv_up[j], mla_w_o[j])
            new_ckv.append(ckv)
            new_kr.append(kr)
        else:
            out, k, v = sb_mixer(h, pos,
                                 None if past_sk is None else past_sk[j],
                                 None if past_sv is None else past_sv[j],
                                 sb_w_in[j], sb_w_o[j])
            new_sk.append(k)
            new_sv.append(v)
        x = x + out
    y = rms_norm(x, final_gain)
    return y, jnp.stack(new_ckv), jnp.stack(new_kr), jnp.stack(new_sk), jnp.stack(new_sv)


def setup_inputs(seed: int = 0) -> dict:
    key = jax.random.key(seed)
    ks = jax.random.split(key, 18)
    f32 = jnp.float32

    def w(k, shape, fan_in):
        return jax.random.normal(k, shape, f32) * (fan_in ** -0.5)

    def gain(k, shape):
        return 1.0 + 0.02 * jax.random.normal(k, shape, f32)

    return {
        "x_prompt": jax.random.normal(ks[0], (BATCH, SEQ, D_MODEL), f32),
        "x_sample": jax.random.normal(ks[1], (DEC_BATCH, DEC_SEQ, D_MODEL), f32),
        "cache_mla_ckv": jax.random.normal(ks[2], (N_MLA, DEC_BATCH, PAST_LEN, KV_LORA), f32),
        "cache_mla_krope": jax.random.normal(ks[3], (N_MLA, DEC_BATCH, PAST_LEN, ROPE_DIM), f32),
        "cache_sb_k": jax.random.normal(ks[4], (N_SB, DEC_BATCH, PAST_LEN, SB_HEADS, SB_HEAD_DIM), f32),
        "cache_sb_v": jax.random.normal(ks[5], (N_SB, DEC_BATCH, PAST_LEN, SB_HEADS, SB_HEAD_DIM), f32),
        "ln_gain": gain(ks[6], (DEPTH, D_MODEL)),
        "final_gain": gain(ks[7], (D_MODEL,)),
        "mla_w_in": w(ks[8], (N_MLA, D_MODEL, MLA_IN), D_MODEL),
        "mla_q_norm": gain(ks[9], (N_MLA, Q_LORA)),
        "mla_kv_norm": gain(ks[10], (N_MLA, KV_LORA)),
        "mla_w_q_up": w(ks[11], (N_MLA, Q_LORA, MLA_HEADS * (NOPE_DIM + ROPE_DIM)), Q_LORA),
        "mla_w_kv_up": w(ks[12], (N_MLA, KV_LORA, MLA_HEADS * (NOPE_DIM + V_DIM)), KV_LORA),
        "mla_w_o": w(ks[13], (N_MLA, MLA_WIDTH, D_MODEL), MLA_WIDTH),
        "sb_w_in": w(ks[14], (N_SB, D_MODEL, SB_IN), D_MODEL),
        "sb_w_o": w(ks[15], (N_SB, SB_WIDTH, D_MODEL), SB_WIDTH),
    }


def reference(x_prompt, x_sample, cache_mla_ckv, cache_mla_krope, cache_sb_k, cache_sb_v,
              ln_gain, final_gain, mla_w_in, mla_q_norm, mla_kv_norm, mla_w_q_up,
              mla_w_kv_up, mla_w_o, sb_w_in, sb_w_o):
    pos_p = jnp.arange(x_prompt.shape[1], dtype=jnp.int32)
    pos_s = cache_mla_ckv.shape[2] + jnp.arange(x_sample.shape[1], dtype=jnp.int32)
    y_prompt, p_ckv, p_kr, p_sk, p_sv = run_trunk(
        x_prompt, pos_p, None, None, None, None, ln_gain, final_gain,
        mla_w_in, mla_q_norm, mla_kv_norm, mla_w_q_up, mla_w_kv_up, mla_w_o, sb_w_in, sb_w_o)
    y_sample, s_ckv, s_kr, s_sk, s_sv = run_trunk(
        x_sample, pos_s, cache_mla_ckv, cache_mla_krope, cache_sb_k, cache_sb_v, ln_gain, final_gain,
        mla_w_in, mla_q_norm, mla_kv_norm, mla_w_q_up, mla_w_kv_up, mla_w_o, sb_w_in, sb_w_o)
    return (y_prompt, y_sample, p_ckv, p_kr, p_sk, p_sv, s_ckv, s_kr, s_sk, s_sv)
```

```python
import functools

import jax
import jax.numpy as jnp
from jax import lax
from jax.experimental import pallas as pl
from jax.experimental.pallas import tpu as pltpu

D_MODEL = 2048
CHUNK = 64
HEADS = 16
HEAD_DIM = 128
Q_LORA = 512
KV_LORA = 512
ROPE_DIM = 64
ROPE_HALF = ROPE_DIM // 2
ROPE_THETA = 10000.0
WIDTH = HEADS * HEAD_DIM
EPS = 1e-6
NEG = -1e30
LANES = 128
MLA_SCALE = (HEAD_DIM + ROPE_DIM) ** -0.5
SB_SCALE = HEAD_DIM ** -0.5
ATTN_BLOCK = 256
VMEM_LIMIT = 56 * 1024 * 1024

BF16 = jnp.bfloat16
F32 = jnp.float32


def _params(*sem):
    return pltpu.CompilerParams(dimension_semantics=sem, vmem_limit_bytes=VMEM_LIMIT)


def _rms(x, g):
    return x * lax.rsqrt(jnp.mean(x * x, axis=-1, keepdims=True) + EPS) * g


def _rope_tile(x, c, s1, s2):
    return x * c + pltpu.roll(x, ROPE_HALF, 1) * s1 + pltpu.roll(x, LANES - ROPE_HALF, 1) * s2


def _dot(a, b):
    return jnp.dot(a, b, preferred_element_type=F32)


def _dot_nt(a, b):
    return lax.dot_general(a, b, (((1,), (1,)), ((), ())), preferred_element_type=F32)


def _mla_front_kernel(x_ref, g_ref, wa_ref, qg_ref, kg_ref, wq_ref, wkv_ref,
                      c_ref, s1_ref, s2_ref,
                      ckv_ref, kr_ref, qn_ref, qr_ref, kn_ref, v_ref):
    h = _rms(x_ref[...], g_ref[...]).astype(BF16)
    a = _dot(h, wa_ref[...])
    c, s1, s2 = c_ref[...], s1_ref[...], s2_ref[...]
    kr_ref[...] = _rope_tile(a[:, Q_LORA + KV_LORA:], c, s1, s2)
    ckv = _rms(a[:, Q_LORA:Q_LORA + KV_LORA], kg_ref[...])
    ckv_ref[...] = ckv
    kv = _dot(ckv.astype(BF16), wkv_ref[...])
    kn_ref[...] = kv[:, :WIDTH].astype(BF16)
    v_ref[...] = kv[:, WIDTH:].astype(BF16)
    cq = _rms(a[:, :Q_LORA], qg_ref[...]).astype(BF16)
    q = _dot(cq, wq_ref[...])
    qn_ref[...] = q[:, :WIDTH].astype(BF16)
    for hh in range(HEADS):
        lo = WIDTH + hh * LANES
        qr_ref[:, hh * LANES:(hh + 1) * LANES] = _rope_tile(q[:, lo:lo + LANES], c, s1, s2).astype(BF16)


def _mla_front(x, g, wa, qg, kg, wq, wkv, tabs, tm):
    m = x.shape[0]
    ntab = tabs[0].shape[0] // tm
    row = lambda i: (i, 0)
    const = lambda i: (0, 0)
    tab = lambda i: (i % ntab, 0)
    return pl.pallas_call(
        _mla_front_kernel,
        grid=(m // tm,),
        in_specs=[
            pl.BlockSpec((tm, D_MODEL), row),
            pl.BlockSpec((1, D_MODEL), const),
            pl.BlockSpec(wa.shape, const),
            pl.BlockSpec((1, Q_LORA), const),
            pl.BlockSpec((1, KV_LORA), const),
            pl.BlockSpec(wq.shape, const),
            pl.BlockSpec(wkv.shape, const),
            pl.BlockSpec((tm, LANES), tab),
            pl.BlockSpec((tm, LANES), tab),
            pl.BlockSpec((tm, LANES), tab),
        ],
        out_specs=[
            pl.BlockSpec((tm, KV_LORA), row),
            pl.BlockSpec((tm, LANES), row),
            pl.BlockSpec((tm, WIDTH), row),
            pl.BlockSpec((tm, WIDTH), row),
            pl.BlockSpec((tm, WIDTH), row),
            pl.BlockSpec((tm, WIDTH), row),
        ],
        out_shape=[
            jax.ShapeDtypeStruct((m, KV_LORA), F32),
            jax.ShapeDtypeStruct((m, LANES), F32),
            jax.ShapeDtypeStruct((m, WIDTH), BF16),
            jax.ShapeDtypeStruct((m, WIDTH), BF16),
            jax.ShapeDtypeStruct((m, WIDTH), BF16),
            jax.ShapeDtypeStruct((m, WIDTH), BF16),
        ],
        compiler_params=_params("parallel"),
        name="mla_front",
    )(x, g, wa, qg, kg, wq, wkv, *tabs)


def _kv_up_kernel(c_ref, w_ref, kn_ref, v_ref):
    kv = _dot(c_ref[...].astype(BF16), w_ref[...])
    kn_ref[...] = kv[:, :WIDTH].astype(BF16)
    v_ref[...] = kv[:, WIDTH:].astype(BF16)


def _kv_up(ckv, wkv, tm):
    m = ckv.shape[0]
    row = lambda i: (i, 0)
    return pl.pallas_call(
        _kv_up_kernel,
        grid=(m // tm,),
        in_specs=[pl.BlockSpec((tm, KV_LORA), row), pl.BlockSpec(wkv.shape, lambda i: (0, 0))],
        out_specs=[pl.BlockSpec((tm, WIDTH), row), pl.BlockSpec((tm, WIDTH), row)],
        out_shape=[jax.ShapeDtypeStruct((m, WIDTH), BF16)] * 2,
        compiler_params=_params("parallel"),
        name="kv_up",
    )(ckv, wkv)


def _norm_matmul_kernel(x_ref, g_ref, w_ref, o_ref, h_ref):
    @pl.when(pl.program_id(1) == 0)
    def _():
        h_ref[...] = _rms(x_ref[...], g_ref[...]).astype(BF16)

    o_ref[...] = _dot(h_ref[...], w_ref[...])


def _norm_matmul(x, g, w, tm, tn):
    m, n = x.shape[0], w.shape[1]
    return pl.pallas_call(
        _norm_matmul_kernel,
        grid=(m // tm, n // tn),
        in_specs=[
            pl.BlockSpec((tm, D_MODEL), lambda i, j: (i, 0)),
            pl.BlockSpec((1, D_MODEL), lambda i, j: (0, 0)),
            pl.BlockSpec((D_MODEL, tn), lambda i, j: (0, j)),
        ],
        out_specs=pl.BlockSpec((tm, tn), lambda i, j: (i, j)),
        out_shape=jax.ShapeDtypeStruct((m, n), F32),
        scratch_shapes=[pltpu.VMEM((tm, D_MODEL), BF16)],
        compiler_params=_params("parallel", "arbitrary"),
        name="norm_matmul",
    )(x, g, w)


def _out_proj_kernel(a_ref, w_ref, x_ref, o_ref):
    o_ref[...] = x_ref[...] + _dot(a_ref[...], w_ref[...])


def _out_proj_final_kernel(a_ref, w_ref, x_ref, g_ref, o_ref):
    o_ref[...] = _rms(x_ref[...] + _dot(a_ref[...], w_ref[...]), g_ref[...])


def _out_proj(a, w, x, final_gain, tm):
    m = x.shape[0]
    row = lambda i: (i, 0)
    const = lambda i: (0, 0)
    in_specs = [pl.BlockSpec((tm, WIDTH), row), pl.BlockSpec(w.shape, const),
                pl.BlockSpec((tm, D_MODEL), row)]
    args = [a, w, x]
    body = _out_proj_kernel
    if final_gain is not None:
        in_specs.append(pl.BlockSpec((1, D_MODEL), const))
        args.append(final_gain)
        body = _out_proj_final_kernel
    return pl.pallas_call(
        body,
        grid=(m // tm,),
        in_specs=in_specs,
        out_specs=pl.BlockSpec((tm, D_MODEL), row),
        out_shape=jax.ShapeDtypeStruct((m, D_MODEL), F32),
        compiler_params=_params("parallel"),
        name="out_proj",
    )(*args)


def _silu(g):
    return g * (1.0 / (1.0 + jnp.exp(-g)))


def _mla_attn_kernel(qn_ref, qr_ref, kdn_ref, kdr_ref, vd_ref, kpn_ref, kpr_ref, vp_ref, gate_ref,
                     o_ref, *, n_prev_static, tk):
    tq = qn_ref.shape[0]
    q = jnp.concatenate([qn_ref[...], qr_ref[...]], axis=-1)

    kd = jnp.concatenate([kdn_ref[...], kdr_ref[...].astype(BF16)], axis=-1)
    s = _dot_nt(q, kd) * MLA_SCALE
    rows = lax.broadcasted_iota(jnp.int32, (tq, tq), 0) // CHUNK
    cols = lax.broadcasted_iota(jnp.int32, (tq, tq), 1) // CHUNK
    s = jnp.where(cols <= rows, s, NEG)
    m0 = jnp.max(s, axis=-1, keepdims=True)
    p = jnp.exp(s - m0)
    l0 = jnp.sum(p, axis=-1, keepdims=True)
    acc0 = _dot(p.astype(BF16), vd_ref[...])

    def body(j, carry):
        m, l, acc = carry
        start = pl.multiple_of(j * tk, tk)
        k = jnp.concatenate([kpn_ref[pl.ds(start, tk), :],
                             kpr_ref[pl.ds(start, tk), :].astype(BF16)], axis=-1)
        s = _dot_nt(q, k) * MLA_SCALE
        m_new = jnp.maximum(m, jnp.max(s, axis=-1, keepdims=True))
        alpha = jnp.exp(m - m_new)
        p = jnp.exp(s - m_new)
        l = l * alpha + jnp.sum(p, axis=-1, keepdims=True)
        acc = acc * alpha + _dot(p.astype(BF16), vp_ref[pl.ds(start, tk), :])
        return m_new, l, acc

    n_prev = pl.program_id(2) if n_prev_static is None else n_prev_static
    m, l, acc = lax.fori_loop(0, n_prev, body, (m0, l0, acc0))
    o_ref[...] = (acc / l * _silu(gate_ref[...])).astype(BF16)


def _mla_attn(qn, qr, kn, kr, v, kpn, kpr, vp, gate, batch, tq, n_prev_static):
    m = qn.shape[0]
    nq = m // batch // tq
    tprev = kpn.shape[1]
    blk = lambda b, h, i: (b * nq + i, h)
    blk0 = lambda b, h, i: (b * nq + i, 0)
    return pl.pallas_call(
        functools.partial(_mla_attn_kernel, n_prev_static=n_prev_static, tk=ATTN_BLOCK),
        grid=(batch, HEADS, nq),
        in_specs=[
            pl.BlockSpec((tq, LANES), blk),
            pl.BlockSpec((tq, LANES), blk),
            pl.BlockSpec((tq, LANES), blk),
            pl.BlockSpec((tq, LANES), blk0),
            pl.BlockSpec((tq, LANES), blk),
            pl.BlockSpec((None, tprev, LANES), lambda b, h, i: (b, 0, h)),
            pl.BlockSpec((None, tprev, LANES), lambda b, h, i: (b, 0, 0)),
            pl.BlockSpec((None, tprev, LANES), lambda b, h, i: (b, 0, h)),
            pl.BlockSpec((tq, LANES), blk),
        ],
        out_specs=pl.BlockSpec((tq, LANES), blk),
        out_shape=jax.ShapeDtypeStruct((m, WIDTH), BF16),
        compiler_params=_params("parallel", "parallel", "arbitrary"),
        name="mla_attn",
    )(qn, qr, kn, kr, v, kpn, kpr, vp, gate)


def _suffix_sums(lf, u):
    hi = lf.astype(BF16)
    lo = (lf - hi.astype(F32)).astype(BF16)
    return _dot(hi, u) + _dot(lo, u)


def _log_fail(z):
    return -(jnp.maximum(z, 0.0) + jnp.log1p(jnp.exp(-jnp.abs(z))))


def _sb_attn_kernel(q_ref, kd_ref, vd_ref, kp_ref, vp_ref, gate_ref, u_ref, o_ref, *, n_prev_static, tk):
    tq = q_ref.shape[0]
    q = q_ref[...].astype(BF16)

    z = _dot_nt(q, kd_ref[...].astype(BF16)) * SB_SCALE
    causal = (lax.broadcasted_iota(jnp.int32, (tq, tq), 1)
              < lax.broadcasted_iota(jnp.int32, (tq, tq), 0))
    lf = jnp.where(causal, _log_fail(z), 0.0)
    incl = _suffix_sums(lf, u_ref[:tq, :tq])
    a = jnp.where(causal, jnp.exp(z + incl), 0.0)
    acc0 = _dot(a.astype(BF16), vd_ref[...].astype(BF16))
    r0 = incl[:, 0:1]

    def body(it, carry):
        r, acc = carry
        j = n_prev - 1 - it
        start = pl.multiple_of(j * tk, tk)
        z = _dot_nt(q, kp_ref[pl.ds(start, tk), :].astype(BF16)) * SB_SCALE
        incl = _suffix_sums(_log_fail(z), u_ref[...])
        a = jnp.exp(z + incl + r)
        acc = acc + _dot(a.astype(BF16), vp_ref[pl.ds(start, tk), :].astype(BF16))
        return r + incl[:, 0:1], acc

    n_prev = pl.program_id(2) if n_prev_static is None else n_prev_static
    _, acc = lax.fori_loop(0, n_prev, body, (r0, acc0))
    o_ref[...] = (acc * _silu(gate_ref[...])).astype(BF16)


def _sb_attn(qkvg, kp, vp, kp_col, vp_col, batch, tq, n_prev_static, u):
    m = qkvg.shape[0]
    nq = m // batch // tq
    tprev = kp.shape[1]

    def col(c):
        return lambda b, h, i: (b * nq + i, c * HEADS + h)

    return pl.pallas_call(
        functools.partial(_sb_attn_kernel, n_prev_static=n_prev_static, tk=ATTN_BLOCK),
        grid=(batch, HEADS, nq),
        in_specs=[
            pl.BlockSpec((tq, LANES), col(0)),
            pl.BlockSpec((tq, LANES), col(1)),
            pl.BlockSpec((tq, LANES), col(2)),
            pl.BlockSpec((None, tprev, LANES), lambda b, h, i: (b, 0, kp_col * HEADS + h)),
            pl.BlockSpec((None, tprev, LANES), lambda b, h, i: (b, 0, vp_col * HEADS + h)),
            pl.BlockSpec((tq, LANES), col(3)),
            pl.BlockSpec((ATTN_BLOCK, ATTN_BLOCK), lambda b, h, i: (0, 0)),
        ],
        out_specs=pl.BlockSpec((tq, LANES), lambda b, h, i: (b * nq + i, h)),
        out_shape=jax.ShapeDtypeStruct((m, WIDTH), BF16),
        compiler_params=_params("parallel", "parallel", "arbitrary"),
        name="sb_attn",
    )(qkvg, qkvg, qkvg, kp, vp, qkvg, u)


def _rope_tables(pos, reps):
    inv = 1.0 / (ROPE_THETA ** (jnp.arange(ROPE_HALF, dtype=F32) * (2.0 / ROPE_DIM)))
    ang = pos.astype(F32)[:, None] * inv[None, :]
    cos, sin = jnp.cos(ang), jnp.sin(ang)
    z32 = jnp.zeros_like(cos)
    z64 = jnp.concatenate([z32, z32], axis=-1)
    c = jnp.concatenate([cos, cos, z64], axis=-1)
    s1 = jnp.concatenate([z32, sin, z64], axis=-1)
    s2 = jnp.concatenate([-sin, z32, z64], axis=-1)
    return tuple(jnp.tile(t, (reps, 1)) for t in (c, s1, s2))


def _prep_mla_weights(w_in, w_q_up, w_kv_up, w_o):
    n_a = Q_LORA + KV_LORA + ROPE_DIM
    wa = jnp.concatenate([w_in[:, :n_a], jnp.zeros((D_MODEL, LANES - ROPE_DIM), F32)], axis=1).astype(BF16)
    wg = w_in[:, n_a:].astype(BF16)
    wq3 = w_q_up.reshape(Q_LORA, HEADS, HEAD_DIM + ROPE_DIM)
    wq_rope = jnp.pad(wq3[:, :, HEAD_DIM:], ((0, 0), (0, 0), (0, LANES - ROPE_DIM)))
    wq = jnp.concatenate([wq3[:, :, :HEAD_DIM].reshape(Q_LORA, WIDTH),
                          wq_rope.reshape(Q_LORA, HEADS * LANES)], axis=1).astype(BF16)
    wkv3 = w_kv_up.reshape(KV_LORA, HEADS, 2 * HEAD_DIM)
    wkv = jnp.concatenate([wkv3[:, :, :HEAD_DIM].reshape(KV_LORA, WIDTH),
                           wkv3[:, :, HEAD_DIM:].reshape(KV_LORA, WIDTH)], axis=1).astype(BF16)
    return wa, wg, wq, wkv, w_o.astype(BF16)


def _trunk(x, batch, pos0, past, ln_gain, final_gain, mla_w, mla_q_norm, mla_kv_norm, sb_w, u):
    rows = x.shape[0]
    t = rows // batch
    tq = min(t, ATTN_BLOCK)
    tm_front = 256
    tabs = _rope_tables(pos0 + jnp.arange(t, dtype=jnp.int32), max(1, tm_front // t))
    new_ckv, new_kr, new_sk, new_sv = [], [], [], []
    depth = ln_gain.shape[0]
    for layer in range(depth):
        g = ln_gain[layer][None, :]
        j = layer // 2
        last = final_gain[None, :] if layer == depth - 1 else None
        if layer % 2 == 0:
            wa, wg, wq, wkv, wo = mla_w[j]
            ckv, kr, qn, qr, kn, v = _mla_front(x, g, wa, mla_q_norm[j][None, :], mla_kv_norm[j][None, :],
                                                wq, wkv, tabs, tm_front)
            gate = _norm_matmul(x, g, wg, min(rows, 1024), 512)
            if past is None:
                kpn, kpr, vp = (kn.reshape(batch, t, WIDTH), kr.reshape(batch, t, LANES),
                                v.reshape(batch, t, WIDTH))
                n_prev = None
            else:
                pc = past["ckv"][j]
                tp = pc.shape[1]
                kpn, vp = _kv_up(pc.reshape(batch * tp, KV_LORA), wkv, 512)
                kpn, vp = kpn.reshape(batch, tp, WIDTH), vp.reshape(batch, tp, WIDTH)
                kpr = jnp.pad(past["kr"][j], ((0, 0), (0, 0), (0, LANES - ROPE_DIM)))
                n_prev = tp // ATTN_BLOCK
            og = _mla_attn(qn, qr, kn, kr, v, kpn, kpr, vp, gate, batch, tq, n_prev)
            new_ckv.append(ckv)
            new_kr.append(kr[:, :ROPE_DIM])
        else:
            w_in, wo = sb_w[j]
            qkvg = _norm_matmul(x, g, w_in, min(rows, 1024), 512)
            if past is None:
                kp = vp = qkvg.reshape(batch, t, 4 * WIDTH)
                kp_col, vp_col, n_prev = 1, 2, None
            else:
                kp = past["sk"][j].reshape(batch, -1, WIDTH)
                vp = past["sv"][j].reshape(batch, -1, WIDTH)
                kp_col, vp_col, n_prev = 0, 0, kp.shape[1] // ATTN_BLOCK
            og = _sb_attn(qkvg, kp, vp, kp_col, vp_col, batch, tq, n_prev, u)
            new_sk.append(qkvg[:, WIDTH:2 * WIDTH])
            new_sv.append(qkvg[:, 2 * WIDTH:3 * WIDTH])
        x = _out_proj(og, wo, x, last, 512)
    return (x.reshape(batch, t, D_MODEL),
            jnp.stack(new_ckv).reshape(-1, batch, t, KV_LORA),
            jnp.stack(new_kr).reshape(-1, batch, t, ROPE_DIM),
            jnp.stack(new_sk).reshape(-1, batch, t, HEADS, HEAD_DIM),
            jnp.stack(new_sv).reshape(-1, batch, t, HEADS, HEAD_DIM))


def kernel(x_prompt, x_sample, cache_mla_ckv, cache_mla_krope, cache_sb_k, cache_sb_v, ln_gain, final_gain, mla_w_in, mla_q_norm, mla_kv_norm, mla_w_q_up, mla_w_kv_up, mla_w_o, sb_w_in, sb_w_o):
    bp, tp, _ = x_prompt.shape
    bs, ts, _ = x_sample.shape
    past_len = cache_mla_ckv.shape[2]
    assert tp % ATTN_BLOCK == 0 and past_len % ATTN_BLOCK == 0 and ts <= ATTN_BLOCK and ts % CHUNK == 0

    mla_w = [_prep_mla_weights(mla_w_in[j], mla_w_q_up[j], mla_w_kv_up[j], mla_w_o[j])
             for j in range(mla_w_in.shape[0])]
    sb_w = [(sb_w_in[j].astype(BF16), sb_w_o[j].astype(BF16)) for j in range(sb_w_in.shape[0])]
    idx = jnp.arange(ATTN_BLOCK, dtype=jnp.int32)
    u = (idx[:, None] >= idx[None, :]).astype(BF16)

    past = {"ckv": cache_mla_ckv, "kr": cache_mla_krope, "sk": cache_sb_k, "sv": cache_sb_v}
    yp, p_ckv, p_kr, p_sk, p_sv = _trunk(x_prompt.reshape(bp * tp, D_MODEL), bp, 0, None, ln_gain, final_gain,
                                         mla_w, mla_q_norm, mla_kv_norm, sb_w, u)
    ys, s_ckv, s_kr, s_sk, s_sv = _trunk(x_sample.reshape(bs * ts, D_MODEL), bs, past_len, past, ln_gain,
                                         final_gain, mla_w, mla_q_norm, mla_kv_norm, sb_w, u)
    return (yp, ys, p_ckv, p_kr, p_sk, p_sv, s_ckv, s_kr, s_sk, s_sv)
```
